```python
import functools
import jax, jax.numpy as jnp
from jax import lax
import numpy as np

D_MODEL = 1024
BATCH = 8
SEQ = 2048
DEPTH = 2
DEC_BATCH = 128
DEC_SEQ = 4
PAST_LEN = 16384
PAGE_SIZE = 128

H_A = 8
NOPE_DIM = 64
ROPE_DIM = 32
V_DIM = 64
Q_LORA = 256
KV_LORA = 128
ROPE_THETA = 10000.0
H_B = 8
G_B = 2
HD_B = 64
CMP_BLOCK = 32
SEL_BLOCK = 64
SEL_TOPK = 16
WINDOW = 512
N_BRANCH = 3
MIX_A = H_A * V_DIM
MIX_B = H_B * HD_B
MIX = MIX_A + MIX_B
IN_COLS = Q_LORA + KV_LORA + ROPE_DIM + H_B * HD_B + 3 * 2 * G_B * HD_B + H_B * N_BRANCH
D_FF = -(-8 * D_MODEL // 768) * 256
QBLK = 128
NEG_INF = -1e30
FORCE = 1e9

kernel_name = 'hymba_mla_nsa_adaln_decode_step'


def rmsnorm(x, g=None, eps=1e-6):
    xf = x.astype(jnp.float32)
    y = xf * lax.rsqrt(jnp.mean(xf * xf, axis=-1, keepdims=True) + eps)
    if g is not None:
        y = y * g.astype(jnp.float32)
    return y.astype(x.dtype)


def rope(x, pos):
    inv = ROPE_THETA ** (-jnp.arange(0, ROPE_DIM, 2, dtype=jnp.float32) / ROPE_DIM)
    ang = pos.astype(jnp.float32)[:, None] * inv[None, :]
    ang = ang.reshape((1, ang.shape[0]) + (1,) * (x.ndim - 3) + (ang.shape[1],))
    cos, sin = jnp.cos(ang), jnp.sin(ang)
    x1, x2 = jnp.split(x.astype(jnp.float32), 2, axis=-1)
    return jnp.concatenate([x1 * cos - x2 * sin, x1 * sin + x2 * cos], axis=-1).astype(x.dtype)


def alibi_slopes():
    return jnp.exp2(-8.0 * (jnp.arange(H_B, dtype=jnp.float32) + 1.0) / H_B)


def masked_softmax(s, mask):
    s = jnp.where(mask, s.astype(jnp.float32), NEG_INF)
    return jnp.where(mask, jax.nn.softmax(s, axis=-1), 0.0)


def pad_seq(a, mult):
    L = a.shape[1]
    Lp = -(-L // mult) * mult
    return jnp.pad(a, [(0, 0), (0, Lp - L)] + [(0, 0)] * (a.ndim - 2))


def over_query_blocks(fn, n_q):
    starts = jnp.arange(n_q // QBLK, dtype=jnp.int32) * QBLK
    out = jnp.moveaxis(lax.map(fn, starts), 0, 1)
    return out.reshape((out.shape[0], n_q) + out.shape[3:])


def mla_attend(q_abs, q_rope, lat, k_rope, q_pos, k_pos):
    scale = (NOPE_DIM + ROPE_DIM) ** -0.5
    s = (jnp.einsum('bthc,blc->bthl', q_abs, lat)
         + jnp.einsum('bthr,blr->bthl', q_rope, k_rope)).astype(jnp.float32) * scale
    mask = (k_pos[None, :] <= q_pos[:, None])[None, :, None, :]
    p = masked_softmax(s, mask).astype(lat.dtype)
    return jnp.einsum('bthl,blc->bthc', p, lat)


def nsa_compress(k, v, pe, w_cmp):
    B, L = k.shape[:2]
    nb = L // CMP_BLOCK
    kb = k.reshape(B, nb, CMP_BLOCK, G_B, HD_B) + pe[0][:, None, :]
    vb = v.reshape(B, nb, CMP_BLOCK, G_B, HD_B) + pe[1][:, None, :]
    kc = jnp.einsum('bnjgd,jde->bnge', kb, w_cmp[0])
    vc = jnp.einsum('bnjgd,jde->bnge', vb, w_cmp[1])
    return kc, vc


def nsa_attend(q, gate, kc, vc, ks, vs, kw, vw, q_pos, kw_pos, slopes):
    B, T = q.shape[:2]
    hg = H_B // G_B
    qg = q.reshape(B, T, G_B, hg, HD_B)
    scale = HD_B ** -0.5
    sl = slopes.reshape(G_B, hg)
    nbc = kc.shape[1]
    c_end = jnp.arange(nbc, dtype=jnp.int32) * CMP_BLOCK + CMP_BLOCK - 1
    dist_c = q_pos[:, None] - c_end[None, :]
    s_c = (jnp.einsum('btghd,bngd->btghn', qg, kc).astype(jnp.float32) * scale
           - sl[None, None, :, :, None] * dist_c[None, :, None, None, :].astype(jnp.float32))
    p_c = masked_softmax(s_c, (dist_c >= 0)[None, :, None, None, :])
    o_c = jnp.einsum('btghn,bngd->btghd', p_c.astype(vc.dtype), vc)
    nbs = ks.shape[1] // SEL_BLOCK
    imp = p_c.sum(axis=3).reshape(B, T, G_B, nbs, SEL_BLOCK // CMP_BLOCK).sum(axis=-1)
    blk = jnp.arange(nbs, dtype=jnp.int32)[None, :]
    cur = (q_pos // SEL_BLOCK)[:, None]
    forced = (blk == 0) | (blk == cur) | (blk == cur - 1)
    avail = blk * SEL_BLOCK <= q_pos[:, None]
    imp = jnp.where(forced[None, :, None, :], FORCE, imp)
    imp = jnp.where(avail[None, :, None, :], imp, NEG_INF)
    top_val, top_idx = lax.top_k(imp, min(SEL_TOPK, nbs))
    blk_ok = top_val > 0.5 * NEG_INF
    ksb = ks.reshape(B, nbs, SEL_BLOCK, G_B, HD_B).transpose(0, 3, 1, 2, 4)
    vsb = vs.reshape(B, nbs, SEL_BLOCK, G_B, HD_B).transpose(0, 3, 1, 2, 4)
    bi = jnp.arange(B)[:, None, None, None]
    gi = jnp.arange(G_B)[None, None, :, None]
    k_g = ksb[bi, gi, top_idx]
    v_g = vsb[bi, gi, top_idx]
    pos_s = top_idx[..., None] * SEL_BLOCK + jnp.arange(SEL_BLOCK, dtype=jnp.int32)
    dist_s = q_pos[None, :, None, None, None] - pos_s
    mask_s = blk_ok[..., None] & (dist_s >= 0)
    s_s = (jnp.einsum('btghd,btgksd->btghks', qg, k_g).astype(jnp.float32) * scale
           - sl[None, None, :, :, None, None] * dist_s[:, :, :, None].astype(jnp.float32))
    kk = top_idx.shape[-1]
    p_s = masked_softmax(s_s.reshape(B, T, G_B, hg, kk * SEL_BLOCK),
                         mask_s[:, :, :, None].reshape(B, T, G_B, 1, kk * SEL_BLOCK))
    p_s = p_s.reshape(B, T, G_B, hg, kk, SEL_BLOCK).astype(v_g.dtype)
    o_s = jnp.einsum('btghks,btgksd->btghd', p_s, v_g)
    dist_w = q_pos[:, None] - kw_pos[None, :]
    mask_w = ((dist_w >= 0) & (dist_w < WINDOW) & (kw_pos[None, :] >= 0))[None, :, None, None, :]
    s_w = (jnp.einsum('btghd,bsgd->btghs', qg, kw).astype(jnp.float32) * scale
           - sl[None, None, :, :, None] * dist_w[None, :, None, None, :].astype(jnp.float32))
    p_w = masked_softmax(s_w, mask_w).astype(vw.dtype)
    o_w = jnp.einsum('btghs,bsgd->btghd', p_w, vw)
    g = gate.reshape(B, T, G_B, hg, N_BRANCH)
    o = g[..., 0:1] * o_c + g[..., 1:2] * o_s + g[..., 2:3] * o_w
    return o.reshape(B, T, H_B * HD_B)


def mixer_inputs(h, pos, lp):
    B, T, _ = h.shape
    z = h @ lp['w_in']
    i1 = Q_LORA
    i2 = i1 + KV_LORA + ROPE_DIM
    i3 = i2 + H_B * HD_B
    i4 = i3 + 3 * 2 * G_B * HD_B
    q_a, kv_a, q_b, kv_b, g = jnp.split(z, [i1, i2, i3, i4], axis=-1)
    q = (rmsnorm(q_a, lp['q_a_g']) @ lp['w_uq']).reshape(B, T, H_A, NOPE_DIM + ROPE_DIM)
    q_rope = rope(q[..., NOPE_DIM:], pos)
    q_abs = jnp.einsum('bthn,chn->bthc', q[..., :NOPE_DIM], lp['w_uk'])
    mla_row = jnp.concatenate([rmsnorm(kv_a[..., :KV_LORA], lp['kv_a_g']),
                               rope(kv_a[..., KV_LORA:], pos)], axis=-1)
    q_b = q_b.reshape(B, T, H_B, HD_B)
    kv_b = kv_b.reshape(B, T, 3, 2, G_B, HD_B)
    gate = jax.nn.sigmoid(g).reshape(B, T, H_B, N_BRANCH)
    return q_abs, q_rope, mla_row, q_b, kv_b, gate


def prompt_mixer(proj, lp, slopes, pos):
    q_abs, q_rope, mla_row, q_b, kv_b, gate = proj
    S = q_b.shape[1]
    lat, k_rope = mla_row[..., :KV_LORA], mla_row[..., KV_LORA:]

    def mla_blk(q0):
        sl = lambda a: lax.dynamic_slice_in_dim(a, q0, QBLK, axis=1)
        return mla_attend(sl(q_abs), sl(q_rope), lat, k_rope, q0 + jnp.arange(QBLK, dtype=jnp.int32), pos)
    o_lat = over_query_blocks(mla_blk, S)

    cmp = pad_seq(kv_b[:, :, 0], SEL_BLOCK)
    sel = pad_seq(kv_b[:, :, 1], SEL_BLOCK)
    kc, vc = nsa_compress(cmp[:, :, 0], cmp[:, :, 1], lp['cmp_pe'], lp['w_cmp'])
    kw_pad = jnp.pad(kv_b[:, :, 2], ((0, 0), (WINDOW, 0), (0, 0), (0, 0), (0, 0)))

    def nsa_blk(q0):
        sl = lambda a: lax.dynamic_slice_in_dim(a, q0, QBLK, axis=1)
        kw = lax.dynamic_slice_in_dim(kw_pad, q0, WINDOW + QBLK, axis=1)
        kw_pos = q0 - WINDOW + jnp.arange(WINDOW + QBLK, dtype=jnp.int32)
        return nsa_attend(sl(q_b), sl(gate), kc, vc, sel[:, :, 0], sel[:, :, 1],
                          kw[:, :, 0], kw[:, :, 1], q0 + jnp.arange(QBLK, dtype=jnp.int32), kw_pos, slopes)
    o_b = over_query_blocks(nsa_blk, S)
    wbp = min(WINDOW, S)
    return o_lat, o_b, (mla_row, kv_b[:, :, 0], kv_b[:, :, 1], kv_b[:, S - wbp:, 2])


def sample_mixer(proj, lp, slopes, pos, cache_mla, cache_cmp, cache_sel, win_state, page_table):
    q_abs, q_rope, mla_row, q_b, kv_b, gate = proj
    DB = q_b.shape[0]
    n_past = page_table.shape[1] * cache_mla.shape[1]

    def past(cache):
        return cache[page_table].reshape((DB, n_past) + cache.shape[2:])
    mla_all = jnp.concatenate([past(cache_mla), mla_row], axis=1)
    k_pos = jnp.arange(mla_all.shape[1], dtype=jnp.int32)
    o_lat = mla_attend(q_abs, q_rope, mla_all[..., :KV_LORA], mla_all[..., KV_LORA:], pos, k_pos)

    cmp_all = pad_seq(jnp.concatenate([past(cache_cmp), kv_b[:, :, 0]], axis=1), SEL_BLOCK)
    sel_all = pad_seq(jnp.concatenate([past(cache_sel), kv_b[:, :, 1]], axis=1), SEL_BLOCK)
    kc, vc = nsa_compress(cmp_all[:, :, 0], cmp_all[:, :, 1], lp['cmp_pe'], lp['w_cmp'])
    wb = win_state.shape[1]
    win = jnp.concatenate([win_state, kv_b[:, :, 2]], axis=1)
    kw_pos = n_past - wb + jnp.arange(win.shape[1], dtype=jnp.int32)
    o_b = nsa_attend(q_b, gate, kc, vc, sel_all[:, :, 0], sel_all[:, :, 1],
                     win[:, :, 0], win[:, :, 1], pos, kw_pos, slopes)
    return o_lat, o_b, (mla_row, kv_b[:, :, 0], kv_b[:, :, 1], win[:, win.shape[1] - wb:])


def block(x, c, pos, lp, mixer):
    B, T, _ = x.shape
    mod = jax.nn.silu(c) @ lp['w_mod'] + lp['b_mod']
    sh1, sc1, g1, sh2, sc2, g2 = [m[:, None, :] for m in jnp.split(mod, 6, axis=-1)]
    h = rmsnorm(x) * (1 + sc1) + sh1
    o_lat, o_b, state = mixer(mixer_inputs(h, pos, lp))
    o_a = jnp.einsum('bthc,chv->bthv', o_lat, lp['w_uv']).reshape(B, T, MIX_A)
    o = jnp.concatenate([rmsnorm(o_a, lp['out_a_g']), rmsnorm(o_b, lp['out_b_g'])], axis=-1)
    x = x + g1 * (o @ lp['w_o'])
    h = rmsnorm(x) * (1 + sc2) + sh2
    x = x + g2 * ((jax.nn.silu(h @ lp['w_ff_gate']) * (h @ lp['w_ff_up'])) @ lp['w_ff_down'])
    return x, state


def setup_inputs(seed: int = 0) -> dict:
    key = jax.random.key(seed)
    ks = jax.random.split(key, 32)
    n_pages = PAST_LEN // PAGE_SIZE
    n_used = DEC_BATCH * n_pages
    n_pool = n_used + n_used // 4
    win_buf = min(WINDOW, PAST_LEN)
    nrm = lambda k, shape, s=1.0: s * jax.random.normal(k, shape, jnp.float32)
    gain = lambda k, shape: 1.0 + 0.02 * jax.random.normal(k, shape, jnp.float32)
    page_table = jax.random.permutation(ks[8], n_pool)[:n_used].reshape(DEC_BATCH, n_pages).astype(jnp.int32)
    return {
        'x_prompt': nrm(ks[0], (BATCH, SEQ, D_MODEL)),
        'x_sample': nrm(ks[1], (DEC_BATCH, DEC_SEQ, D_MODEL)),
        'c_prompt': nrm(ks[2], (BATCH, D_MODEL)),
        'c_sample': nrm(ks[3], (DEC_BATCH, D_MODEL)),
        'cache_mla': nrm(ks[4], (DEPTH, n_pool, PAGE_SIZE, KV_LORA + ROPE_DIM)),
        'cache_nsa_cmp': nrm(ks[5], (DEPTH, n_pool, PAGE_SIZE, 2, G_B, HD_B)),
        'cache_nsa_sel': nrm(ks[6], (DEPTH, n_pool, PAGE_SIZE, 2, G_B, HD_B)),
        'state_nsa_win': nrm(ks[7], (DEPTH, DEC_BATCH, win_buf, 2, G_B, HD_B)),
        'page_table': page_table,
        'w_mod': nrm(ks[9], (DEPTH, D_MODEL, 6 * D_MODEL), 0.5 * D_MODEL ** -0.5),
        'b_mod': nrm(ks[10], (DEPTH, 6 * D_MODEL), 0.02),
        'w_in': nrm(ks[11], (DEPTH, D_MODEL, IN_COLS), D_MODEL ** -0.5),
        'q_a_g': gain(ks[12], (DEPTH, Q_LORA)),
        'kv_a_g': gain(ks[13], (DEPTH, KV_LORA)),
        'w_uq': nrm(ks[14], (DEPTH, Q_LORA, H_A * (NOPE_DIM + ROPE_DIM)), Q_LORA ** -0.5),
        'w_uk': nrm(ks[15], (DEPTH, KV_LORA, H_A, NOPE_DIM), KV_LORA ** -0.5),
        'w_uv': nrm(ks[16], (DEPTH, KV_LORA, H_A, V_DIM), KV_LORA ** -0.5),
        'cmp_pe': nrm(ks[17], (DEPTH, 2, CMP_BLOCK, HD_B), 0.02),
        'w_cmp': nrm(ks[18], (DEPTH, 2, CMP_BLOCK, HD_B, HD_B), (CMP_BLOCK * HD_B) ** -0.5),
        'out_a_g': gain(ks[19], (DEPTH, MIX_A)),
        'out_b_g': gain(ks[20], (DEPTH, MIX_B)),
        'w_o': nrm(ks[21], (DEPTH, MIX, D_MODEL), MIX ** -0.5),
        'w_ff_gate': nrm(ks[22], (DEPTH, D_MODEL, D_FF), D_MODEL ** -0.5),
        'w_ff_up': nrm(ks[23], (DEPTH, D_MODEL, D_FF), D_MODEL ** -0.5),
        'w_ff_down': nrm(ks[24], (DEPTH, D_FF, D_MODEL), D_FF ** -0.5),
        'final_g': gain(ks[25], (D_MODEL,)),
    }


def reference(x_prompt, x_sample, c_prompt, c_sample, cache_mla, cache_nsa_cmp, cache_nsa_sel,
              state_nsa_win, page_table, w_mod, b_mod, w_in, q_a_g, kv_a_g, w_uq, w_uk, w_uv,
              cmp_pe, w_cmp, out_a_g, out_b_g, w_o, w_ff_gate, w_ff_up, w_ff_down, final_g):
    slopes = alibi_slopes()
    n_past = page_table.shape[1] * cache_mla.shape[2]
    pos_p = jnp.arange(x_prompt.shape[1], dtype=jnp.int32)
    pos_s = n_past + jnp.arange(x_sample.shape[1], dtype=jnp.int32)
    xp, xs = x_prompt, x_sample
    st_p, st_s = [], []
    for l in range(DEPTH):
        lp = {'w_mod': w_mod[l], 'b_mod': b_mod[l], 'w_in': w_in[l], 'q_a_g': q_a_g[l],
              'kv_a_g': kv_a_g[l], 'w_uq': w_uq[l], 'w_uk': w_uk[l], 'w_uv': w_uv[l],
              'cmp_pe': cmp_pe[l], 'w_cmp': w_cmp[l], 'out_a_g': out_a_g[l], 'out_b_g': out_b_g[l],
              'w_o': w_o[l], 'w_ff_gate': w_ff_gate[l], 'w_ff_up': w_ff_up[l], 'w_ff_down': w_ff_down[l]}
        xp, sp = block(xp, c_prompt, pos_p, lp,
                       functools.partial(prompt_mixer, lp=lp, slopes=slopes, pos=pos_p))
        xs, ss = block(xs, c_sample, pos_s, lp,
                       functools.partial(sample_mixer, lp=lp, slopes=slopes, pos=pos_s,
                                         cache_mla=cache_mla[l], cache_cmp=cache_nsa_cmp[l],
                                         cache_sel=cache_nsa_sel[l], win_state=state_nsa_win[l],
                                         page_table=page_table))
        st_p.append(sp)
        st_s.append(ss)
    stack = lambda sts, i: jnp.stack([s[i] for s in sts], axis=0)
    y_prompt = rmsnorm(xp, final_g)
    y_sample = rmsnorm(xs, final_g)
    return (y_prompt, y_sample, stack(st_p, 0), stack(st_s, 0), stack(st_p, 1), stack(st_s, 1),
            stack(st_p, 2), stack(st_s, 2), stack(st_p, 3), stack(st_s, 3))
```

```python
import functools

import jax
import jax.numpy as jnp
from jax import lax
from jax.experimental import pallas as pl
from jax.experimental.pallas import tpu as pltpu

H_A = 8
NOPE_DIM = 64
ROPE_DIM = 32
V_DIM = 64
Q_LORA = 256
KV_LORA = 128
ROPE_THETA = 10000.0
H_B = 8
G_B = 2
HD_B = 64
CMP_BLOCK = 32
SEL_BLOCK = 64
SEL_TOPK = 16
WINDOW = 512
N_BRANCH = 3
QBLK = 128
NEG_INF = -1e30
FORCE = 1e9
EPS = 1e-6
HG = H_B // G_B
MLA_W = KV_LORA + ROPE_DIM
KV_W = 2 * G_B * HD_B
LANES = 128
T8 = 8
C_QA, C_LAT, C_KR, C_KRR, C_QB, C_CMP, C_SEL, C_WIN, C_GATE, C_END = (
    0, 256, 384, 512, 640, 1152, 1408, 1664, 1920, 2048)
VMEM_LIMIT = 56 * 1024 * 1024

_BF = jnp.bfloat16
_F32 = jnp.float32


def _dot(a, b):
    return jnp.dot(a, b, preferred_element_type=_F32)


def _dot_nt(a, b):
    return lax.dot_general(a, b, (((1,), (1,)), ((), ())), preferred_element_type=_F32)


def _cparams(*sem):
    return pltpu.CompilerParams(dimension_semantics=sem, vmem_limit_bytes=VMEM_LIMIT)


def _alibi_slope(h):
    return 2.0 ** (-8.0 * (h + 1.0) / H_B)


def _mod_kernel(c_ref, w_ref, b_ref, o_ref):
    c = c_ref[...]
    a = c * jax.nn.sigmoid(c)
    o_ref[...] = _dot(a.astype(_BF), w_ref[...].astype(_BF)) + b_ref[...]


def _mod(c_all, w_mod, b_mod, layer):
    R, D = c_all.shape
    N = w_mod.shape[2]
    tn = N // 4
    return pl.pallas_call(
        _mod_kernel,
        grid=(N // tn,),
        in_specs=[pl.BlockSpec((R, D), lambda j: (0, 0)),
                  pl.BlockSpec((None, D, tn), lambda j: (layer, 0, j)),
                  pl.BlockSpec((None, 1, tn), lambda j: (layer, 0, j))],
        out_specs=pl.BlockSpec((R, tn), lambda j: (0, j)),
        out_shape=jax.ShapeDtypeStruct((R, N), _F32),
        compiler_params=_cparams("arbitrary"),
    )(c_all, w_mod, b_mod.reshape(b_mod.shape[0], 1, N))


def _fold_kernel(a_ref, b_ref, o_ref):
    o_ref[0] = lax.dot_general(a_ref[0], b_ref[0], (((1,), (1,)), ((), ())),
                               precision=lax.Precision.HIGHEST, preferred_element_type=_F32)


def _fold_q_absorb(nope_t, w_uk_t):
    return pl.pallas_call(
        _fold_kernel,
        grid=(H_A,),
        in_specs=[pl.BlockSpec((1, Q_LORA, NOPE_DIM), lambda h: (h, 0, 0)),
                  pl.BlockSpec((1, KV_LORA, NOPE_DIM), lambda h: (h, 0, 0))],
        out_specs=pl.BlockSpec((1, Q_LORA, KV_LORA), lambda h: (h, 0, 0)),
        out_shape=jax.ShapeDtypeStruct((H_A, Q_LORA, KV_LORA), _F32),
        compiler_params=_cparams("arbitrary"),
    )(nope_t, w_uk_t)


def _proj_kernel(x_ref, sc_ref, sh_ref, cs_ref, sn_ref, win_ref, qg_ref, kvg_ref, wq_ref,
                 qmla_ref, mla_ref, qb_ref, cmp_ref, sel_ref, wkv_ref, gate_ref):
    x = x_ref[...]
    h = x * lax.rsqrt(jnp.mean(x * x, axis=-1, keepdims=True) + EPS) * (1.0 + sc_ref[0]) + sh_ref[0]
    z = _dot(h.astype(_BF), win_ref[...])
    qa = z[:, C_QA:C_QA + Q_LORA]
    qn = qa * lax.rsqrt(jnp.mean(qa * qa, axis=-1, keepdims=True) + EPS) * qg_ref[...]
    qq = _dot(qn.astype(_BF), wq_ref[...])
    cs = cs_ref[...]
    sn = sn_ref[...]
    n_abs = H_A * KV_LORA
    n_rope = H_A * ROPE_DIM
    cs2 = jnp.concatenate([cs, cs], axis=1)
    sn2 = jnp.concatenate([sn, sn], axis=1)
    roped = qq[:, n_abs:n_abs + n_rope] * cs2 + qq[:, n_abs + n_rope:n_abs + 2 * n_rope] * sn2
    for hh in range(H_A):
        qmla_ref[hh, :, 0:KV_LORA] = qq[:, hh * KV_LORA:(hh + 1) * KV_LORA].astype(_BF)
        qmla_ref[hh, :, KV_LORA:MLA_W] = roped[:, hh * ROPE_DIM:(hh + 1) * ROPE_DIM].astype(_BF)
    lat = z[:, C_LAT:C_LAT + KV_LORA]
    mla_ref[:, 0:KV_LORA] = lat * lax.rsqrt(jnp.mean(lat * lat, axis=-1, keepdims=True) + EPS) * kvg_ref[...]
    mla_ref[:, KV_LORA:MLA_W] = (z[:, C_KR:C_KR + ROPE_DIM] * cs[:, 0:ROPE_DIM]
                                 + z[:, C_KRR:C_KRR + ROPE_DIM] * sn[:, 0:ROPE_DIM])
    qb_ref[...] = z[:, C_QB:C_QB + H_B * HD_B].astype(_BF)
    cmp_ref[...] = z[:, C_CMP:C_CMP + KV_W]
    sel_ref[...] = z[:, C_SEL:C_SEL + KV_W]
    wkv_ref[...] = z[:, C_WIN:C_WIN + KV_W]
    gate_ref[...] = jax.nn.sigmoid(z[:, C_GATE:C_END])


def _proj(x, sc, sh, cs, sn, w_in_p, q_a_g, kv_a_g, wq_all, tm, mod_map, tab_map):
    N, D = x.shape
    mod_blk = (1, 1 if sc.shape[1] == 1 else tm, D)
    const = lambda i: (0, 0)
    row = lambda i: (i, 0)
    outs = [
        (jax.ShapeDtypeStruct((H_A, N, MLA_W), _BF), pl.BlockSpec((H_A, tm, MLA_W), lambda i: (0, i, 0))),
        (jax.ShapeDtypeStruct((N, MLA_W), _F32), pl.BlockSpec((tm, MLA_W), row)),
        (jax.ShapeDtypeStruct((N, H_B * HD_B), _BF), pl.BlockSpec((tm, H_B * HD_B), row)),
        (jax.ShapeDtypeStruct((N, KV_W), _F32), pl.BlockSpec((tm, KV_W), row)),
        (jax.ShapeDtypeStruct((N, KV_W), _F32), pl.BlockSpec((tm, KV_W), row)),
        (jax.ShapeDtypeStruct((N, KV_W), _F32), pl.BlockSpec((tm, KV_W), row)),
        (jax.ShapeDtypeStruct((N, LANES), _F32), pl.BlockSpec((tm, LANES), row)),
    ]
    return pl.pallas_call(
        _proj_kernel,
        grid=(N // tm,),
        in_specs=[pl.BlockSpec((tm, D), row),
                  pl.BlockSpec(mod_blk, mod_map),
                  pl.BlockSpec(mod_blk, mod_map),
                  pl.BlockSpec((tm, LANES), tab_map),
                  pl.BlockSpec((tm, LANES), tab_map),
                  pl.BlockSpec(w_in_p.shape, const),
                  pl.BlockSpec(q_a_g.shape, const),
                  pl.BlockSpec(kv_a_g.shape, const),
                  pl.BlockSpec(wq_all.shape, const)],
        out_specs=[o[1] for o in outs],
        out_shape=[o[0] for o in outs],
        compiler_params=_cparams("arbitrary"),
    )(x, sc, sh, cs, sn, w_in_p, q_a_g, kv_a_g, wq_all)


def _masked_softmax(s, mask):
    s = jnp.where(mask, s, NEG_INF)
    m = jnp.max(s, axis=-1, keepdims=True)
    e = jnp.where(mask, jnp.exp(s - m), 0.0)
    l = jnp.sum(e, axis=-1, keepdims=True)
    return e, l


def _norm_rows(e_dot_v, l):
    return jnp.where(l > 0.0, e_dot_v / jnp.where(l > 0.0, l, 1.0), 0.0)


def _pad_group_queries(qb, g, rows):
    lane = lax.broadcasted_iota(jnp.int32, (rows, LANES), 1)
    keep = (lane // HD_B) == g
    qf = qb.astype(_F32)
    return jnp.concatenate(
        [jnp.where(keep, qf[:, m * LANES:(m + 1) * LANES], 0.0) for m in range(HG)], axis=0).astype(_BF)


def _row_consts(g, rows):
    ridx = lax.broadcasted_iota(jnp.int32, (HG * rows, 1), 0)
    m = ridx // rows
    slope = jnp.zeros((HG * rows, 1), _F32)
    for mm in range(HG):
        slope = jnp.where(m == mm, _alibi_slope(g * HG + mm), slope)
    return slope, ridx % rows


def _topk_select(v, n_iter):
    R, L = v.shape
    jf = lax.broadcasted_iota(jnp.int32, (R, L), 1).astype(_F32)
    sel = jnp.zeros((R, L), _F32)
    picks = []
    for _ in range(n_iter):
        mx = jnp.max(v, axis=-1, keepdims=True)
        idx = jnp.min(jnp.where(v == mx, jf, 1e9), axis=-1, keepdims=True)
        hit = jf == idx
        sel = jnp.where(hit & (mx > 0.5 * NEG_INF), 1.0, sel)
        v = jnp.where(hit, -3e38, v)
        picks.append(idx)
    return sel, picks


def _compress_rows(load_rows, pe_ref, w_ref, nsb):
    acc = jnp.zeros((2 * nsb, KV_W), _F32)
    for j in range(CMP_BLOCK):
        pe = pe_ref[j:j + 1, :]
        xe = jnp.concatenate([load_rows(0, j, SEL_BLOCK), load_rows(1, j, SEL_BLOCK)], axis=1) + pe
        xo = jnp.concatenate([load_rows(0, CMP_BLOCK + j, SEL_BLOCK),
                              load_rows(1, CMP_BLOCK + j, SEL_BLOCK)], axis=1) + pe
        lhs = jnp.concatenate([xe, xo], axis=0).astype(_BF)
        acc = acc + _dot(lhs, w_ref[j])
    return acc


def _cmp_branch(qp, kcat, slope, q_pos, nsb):
    kk = kcat[:, 0:LANES].astype(_BF)
    vv = kcat[:, LANES:KV_W].astype(_BF)
    s = _dot_nt(qp, kk) * (HD_B ** -0.5)
    col = lax.broadcasted_iota(jnp.int32, s.shape, 1)
    blk = jnp.where(col < nsb, 2 * col, 2 * (col - nsb) + 1)
    dist = q_pos - (blk * CMP_BLOCK + CMP_BLOCK - 1)
    s = s - slope * dist.astype(_F32)
    e, l = _masked_softmax(s, dist >= 0)
    p = _norm_rows(e, l)
    return _dot(p.astype(_BF), vv), p


def _finish_nsa(o_heads, gain_ref, o_ref, rows):
    lane = lax.broadcasted_iota(jnp.int32, (rows, LANES), 1)
    tiles = [jnp.where(lane < HD_B, o_heads[0][m], o_heads[1][m]) for m in range(HG)]
    ss = sum(jnp.sum(t * t, axis=-1, keepdims=True) for t in tiles)
    r = lax.rsqrt(ss / (H_B * HD_B) + EPS)
    for m in range(HG):
        o_ref[:, m * LANES:(m + 1) * LANES] = (tiles[m] * r * gain_ref[:, m * LANES:(m + 1) * LANES]).astype(o_ref.dtype)


def _finish_mla(o_lat, rows, wuv_ref, gain_ref, o_ref):
    tiles = []
    for m in range(H_A // 2):
        a = o_lat[(2 * m) * rows:(2 * m + 1) * rows].astype(_BF)
        b = o_lat[(2 * m + 1) * rows:(2 * m + 2) * rows].astype(_BF)
        tiles.append(_dot(a, wuv_ref[2 * m]) + _dot(b, wuv_ref[2 * m + 1]))
    ss = sum(jnp.sum(t * t, axis=-1, keepdims=True) for t in tiles)
    r = lax.rsqrt(ss / (H_A * V_DIM) + EPS)
    for m in range(H_A // 2):
        o_ref[:, m * LANES:(m + 1) * LANES] = (tiles[m] * r * gain_ref[:, m * LANES:(m + 1) * LANES]).astype(o_ref.dtype)


def _mla_p_kernel(q_ref, kv_ref, wuv_ref, g_ref, o_ref, m_scr, l_scr, acc_scr, *, kc):
    qi = pl.program_id(1)
    q0 = qi * QBLK
    R = H_A * QBLK
    q = q_ref[...].reshape(R, MLA_W)
    t_row = q0 + lax.broadcasted_iota(jnp.int32, (R, 1), 0) % QBLK
    m_scr[...] = jnp.full((R, 1), NEG_INF, _F32)
    l_scr[...] = jnp.zeros((R, 1), _F32)
    acc_scr[...] = jnp.zeros((R, KV_LORA), _F32)
    scale = (NOPE_DIM + ROPE_DIM) ** -0.5

    def body(c, carry):
        k0 = pl.multiple_of(c * kc, kc)
        k = kv_ref[pl.ds(k0, kc), :].astype(_BF)
        s = _dot_nt(q, k) * scale
        col = k0 + lax.broadcasted_iota(jnp.int32, (R, kc), 1)
        mask = col <= t_row
        s = jnp.where(mask, s, NEG_INF)
        m_old = m_scr[...]
        m_new = jnp.maximum(m_old, jnp.max(s, axis=-1, keepdims=True))
        alpha = jnp.exp(m_old - m_new)
        p = jnp.where(mask, jnp.exp(s - m_new), 0.0)
        l_scr[...] = alpha * l_scr[...] + jnp.sum(p, axis=-1, keepdims=True)
        acc_scr[...] = alpha * acc_scr[...] + _dot(p.astype(_BF), k[:, 0:KV_LORA])
        m_scr[...] = m_new
        return carry

    lax.fori_loop(0, (q0 + QBLK + kc - 1) // kc, body, 0)
    o_lat = _norm_rows(acc_scr[...], l_scr[...])
    _finish_mla(o_lat, QBLK, wuv_ref, g_ref, o_ref)


def _mla_prompt(qmla, mla_row, wuv_p, gain, B, S):
    kc = 512 if S % 512 == 0 else (256 if S % 256 == 0 else 128)
    nq = S // QBLK
    R = H_A * QBLK
    return pl.pallas_call(
        functools.partial(_mla_p_kernel, kc=kc),
        grid=(B, nq),
        in_specs=[pl.BlockSpec((H_A, QBLK, MLA_W), lambda b, i: (0, b * nq + i, 0)),
                  pl.BlockSpec((S, MLA_W), lambda b, i: (b, 0)),
                  pl.BlockSpec(wuv_p.shape, lambda b, i: (0, 0, 0)),
                  pl.BlockSpec(gain.shape, lambda b, i: (0, 0))],
        out_specs=pl.BlockSpec((QBLK, H_A * V_DIM), lambda b, i: (b * nq + i, 0)),
        out_shape=jax.ShapeDtypeStruct((B * S, H_A * V_DIM), _BF),
        scratch_shapes=[pltpu.VMEM((R, 1), _F32), pltpu.VMEM((R, 1), _F32), pltpu.VMEM((R, KV_LORA), _F32)],
        compiler_params=_cparams("arbitrary", "arbitrary"),
    )(qmla, mla_row, wuv_p, gain)


def _compress_p_kernel(k_ref, v_ref, pe_ref, w_ref, o_ref, *, nsb):
    halves = (k_ref, v_ref)
    acc = _compress_rows(lambda half, start, stride: halves[half][pl.ds(start, nsb, stride=stride), :],
                         pe_ref, w_ref, nsb)
    o_ref[0, 0] = acc[0:nsb]
    o_ref[0, 1] = acc[nsb:2 * nsb]


def _compress_prompt(cmp_rows, pe_rows, w_bd, B, S):
    nsb = S // SEL_BLOCK
    return pl.pallas_call(
        functools.partial(_compress_p_kernel, nsb=nsb),
        grid=(B,),
        in_specs=[pl.BlockSpec((S, LANES), lambda b: (b, 0)),
                  pl.BlockSpec((S, LANES), lambda b: (b, 1)),
                  pl.BlockSpec(pe_rows.shape, lambda b: (0, 0)),
                  pl.BlockSpec(w_bd.shape, lambda b: (0, 0, 0))],
        out_specs=pl.BlockSpec((1, 2, nsb, KV_W), lambda b: (b, 0, 0, 0)),
        out_shape=jax.ShapeDtypeStruct((B, 2, nsb, KV_W), _F32),
        compiler_params=_cparams("arbitrary"),
    )(cmp_rows, cmp_rows, pe_rows, w_bd)


def _nsa_p_kernel(qb_ref, gate_ref, kc_ref, sel_ref, win_ref, gain_ref, o_ref,
                  mask_scr, m_scr, l_scr, acc_scr, *, S, kc):
    qi = pl.program_id(1)
    q0 = qi * QBLK
    nsb = S // SEL_BLOCK
    n_chunks = S // kc
    R = HG * QBLK
    scale = HD_B ** -0.5
    qb = qb_ref[...]
    gate = gate_ref[...]
    t_tok = q0 + lax.broadcasted_iota(jnp.int32, (QBLK, 1), 0)
    kcat = jnp.concatenate([kc_ref[0, 0], kc_ref[0, 1]], axis=0)
    n_used = (q0 + QBLK + kc - 1) // kc
    wstart = pl.multiple_of(jnp.maximum(q0 - WINDOW, 0), QBLK)
    wlen = WINDOW + QBLK
    o_heads = []
    for g in range(G_B):
        qp = _pad_group_queries(qb, g, QBLK)
        slope, r_tok = _row_consts(g, QBLK)
        t_row = q0 + r_tok
        o_c, p_c = _cmp_branch(qp, kcat, slope, t_row, nsb)
        ph = p_c[0:QBLK]
        for m in range(1, HG):
            ph = ph + p_c[m * QBLK:(m + 1) * QBLK]
        imp = ph[:, 0:nsb] + ph[:, nsb:2 * nsb]

        @pl.when(q0 + QBLK > SEL_TOPK * SEL_BLOCK)
        def _():
            j = lax.broadcasted_iota(jnp.int32, (QBLK, nsb), 1)
            cur = t_tok // SEL_BLOCK
            forced = (j == 0) | (j == cur) | (j == cur - 1)
            v = jnp.where(forced, FORCE, imp)
            v = jnp.where(j * SEL_BLOCK <= t_tok, v, NEG_INF)
            sel, _ = _topk_select(v, min(SEL_TOPK, nsb))
            selb = sel.astype(_BF)
            for c in range(n_chunks):
                kblk = (c * kc + lax.broadcasted_iota(jnp.int32, (nsb, kc), 1)) // SEL_BLOCK
                expand = (kblk == lax.broadcasted_iota(jnp.int32, (nsb, kc), 0)).astype(_BF)
                mask_scr[c] = _dot(selb, expand)

        @pl.when(q0 + QBLK <= SEL_TOPK * SEL_BLOCK)
        def _():
            mask_scr[...] = jnp.ones(mask_scr.shape, _F32)

        m_scr[...] = jnp.full((R, 1), NEG_INF, _F32)
        l_scr[...] = jnp.zeros((R, 1), _F32)
        acc_scr[...] = jnp.zeros((R, LANES), _F32)

        def body(c, carry):
            k0 = pl.multiple_of(c * kc, kc)
            kv = sel_ref[pl.ds(k0, kc), :]
            s = _dot_nt(qp, kv[:, 0:LANES].astype(_BF)) * scale
            dist = t_row - (k0 + lax.broadcasted_iota(jnp.int32, (R, kc), 1))
            s = s - slope * dist.astype(_F32)
            mk = mask_scr[c]
            mk = jnp.concatenate([mk] * HG, axis=0)
            mask = (mk > 0.5) & (dist >= 0)
            s = jnp.where(mask, s, NEG_INF)
            m_old = m_scr[...]
            m_new = jnp.maximum(m_old, jnp.max(s, axis=-1, keepdims=True))
            alpha = jnp.exp(m_old - m_new)
            p = jnp.where(mask, jnp.exp(s - m_new), 0.0)
            l_scr[...] = alpha * l_scr[...] + jnp.sum(p, axis=-1, keepdims=True)
            acc_scr[...] = alpha * acc_scr[...] + _dot(p.astype(_BF), kv[:, LANES:KV_W].astype(_BF))
            m_scr[...] = m_new
            return carry

        lax.fori_loop(0, n_used, body, 0)
        o_s = _norm_rows(acc_scr[...], l_scr[...])

        kw = win_ref[pl.ds(wstart, wlen), :]
        s = _dot_nt(qp, kw[:, 0:LANES].astype(_BF)) * scale
        dist = t_row - (wstart + lax.broadcasted_iota(jnp.int32, (R, wlen), 1))
        s = s - slope * dist.astype(_F32)
        e, l = _masked_softmax(s, (dist >= 0) & (dist < WINDOW))
        o_w = _norm_rows(_dot(e.astype(_BF), kw[:, LANES:KV_W].astype(_BF)), l)

        heads = []
        for m in range(HG):
            hh = g * HG + m
            rs = slice(m * QBLK, (m + 1) * QBLK)
            heads.append(gate[:, hh:hh + 1] * o_c[rs]
                         + gate[:, H_B + hh:H_B + hh + 1] * o_s[rs]
                         + gate[:, 2 * H_B + hh:2 * H_B + hh + 1] * o_w[rs])
        o_heads.append(heads)
    _finish_nsa(o_heads, gain_ref, o_ref, QBLK)


def _nsa_prompt(qb, gate, kc_p, sel_rows, win_rows, gain, B, S):
    kc = 512 if S % 512 == 0 else (256 if S % 256 == 0 else 128)
    nq = S // QBLK
    nsb = S // SEL_BLOCK
    R = HG * QBLK
    rowmap = lambda b, i: (b * nq + i, 0)
    return pl.pallas_call(
        functools.partial(_nsa_p_kernel, S=S, kc=kc),
        grid=(B, nq),
        in_specs=[pl.BlockSpec((QBLK, H_B * HD_B), rowmap),
                  pl.BlockSpec((QBLK, LANES), rowmap),
                  pl.BlockSpec((1, 2, nsb, KV_W), lambda b, i: (b, 0, 0, 0)),
                  pl.BlockSpec((S, KV_W), lambda b, i: (b, 0)),
                  pl.BlockSpec((S, KV_W), lambda b, i: (b, 0)),
                  pl.BlockSpec(gain.shape, lambda b, i: (0, 0))],
        out_specs=pl.BlockSpec((QBLK, H_B * HD_B), rowmap),
        out_shape=jax.ShapeDtypeStruct((B * S, H_B * HD_B), _BF),
        scratch_shapes=[pltpu.VMEM((S // kc, QBLK, kc), _F32),
                        pltpu.VMEM((R, 1), _F32), pltpu.VMEM((R, 1), _F32), pltpu.VMEM((R, LANES), _F32)],
        compiler_params=_cparams("arbitrary", "arbitrary"),
    )(qb, gate, kc_p, sel_rows, win_rows, gain)


def _page_copies(cache_ref, layer, page, buf_ref, slot, dst_row, sem_ref):
    rows = cache_ref.shape[2]
    if len(buf_ref.shape) == 3:
        return [pltpu.make_async_copy(cache_ref.at[layer, page], buf_ref.at[slot, pl.ds(dst_row, rows)],
                                      sem_ref.at[slot])]
    return [pltpu.make_async_copy(cache_ref.at[layer, page, :, pl.ds(half * LANES, LANES)],
                                  buf_ref.at[slot, half, pl.ds(dst_row, rows)], sem_ref.at[slot])
            for half in range(buf_ref.shape[1])]


def _start_pages(pt_ref, cache_ref, layer, buf_ref, sem_ref, step, slot, ppc):
    rows = cache_ref.shape[2]

    def issue(p, carry):
        page = pt_ref[step * ppc + p]
        for cp in _page_copies(cache_ref, layer, page, buf_ref, slot, pl.multiple_of(p * rows, rows), sem_ref):
            cp.start()
        return carry

    lax.fori_loop(0, ppc, issue, 0)


def _wait_pages(cache_ref, layer, buf_ref, sem_ref, slot, ppc):
    def wait(p, carry):
        for cp in _page_copies(cache_ref, layer, 0, buf_ref, slot, 0, sem_ref):
            cp.wait()
        return carry

    lax.fori_loop(0, ppc, wait, 0)


def _paged_prologue(pt_ref, cache_ref, layer, buf_ref, sem_ref, ppc):
    step = pl.program_id(0) * pl.num_programs(1) + pl.program_id(1)
    total = pl.num_programs(0) * pl.num_programs(1)
    slot = step % 2

    @pl.when(step == 0)
    def _():
        _start_pages(pt_ref, cache_ref, layer, buf_ref, sem_ref, step, slot, ppc)

    @pl.when(step + 1 < total)
    def _():
        _start_pages(pt_ref, cache_ref, layer, buf_ref, sem_ref, step + 1, 1 - slot, ppc)

    _wait_pages(cache_ref, layer, buf_ref, sem_ref, slot, ppc)
    return slot


def _mla_s_kernel(pt_ref, q_ref, cache_ref, new_ref, wuv_ref, g_ref, o_ref,
                  buf, sem, m_scr, l_scr, acc_scr, *, layer, ppc, sub, n_past, T):
    c = pl.program_id(1)
    R = H_A * T8
    slot = _paged_prologue(pt_ref, cache_ref, layer, buf, sem, ppc)
    q = q_ref[0]
    scale = (NOPE_DIM + ROPE_DIM) ** -0.5

    @pl.when(c == 0)
    def _():
        m_scr[...] = jnp.full((R, 1), NEG_INF, _F32)
        l_scr[...] = jnp.zeros((R, 1), _F32)
        acc_scr[...] = jnp.zeros((R, KV_LORA), _F32)

    def update(k, mask):
        s = _dot_nt(q, k) * scale
        if mask is not None:
            s = jnp.where(mask, s, NEG_INF)
        m_old = m_scr[...]
        m_new = jnp.maximum(m_old, jnp.max(s, axis=-1, keepdims=True))
        alpha = jnp.exp(m_old - m_new)
        p = jnp.exp(s - m_new)
        if mask is not None:
            p = jnp.where(mask, p, 0.0)
        l_scr[...] = alpha * l_scr[...] + jnp.sum(p, axis=-1, keepdims=True)
        acc_scr[...] = alpha * acc_scr[...] + _dot(p.astype(_BF), k[:, 0:KV_LORA])
        m_scr[...] = m_new

    rows = ppc * cache_ref.shape[2]
    for i in range(rows // sub):
        update(buf[slot, pl.ds(i * sub, sub), :].astype(_BF), None)

    @pl.when(c == pl.num_programs(1) - 1)
    def _():
        t_row = lax.broadcasted_iota(jnp.int32, (R, T8), 0) % T8
        col = lax.broadcasted_iota(jnp.int32, (R, T8), 1)
        update(new_ref[0].astype(_BF), (col <= t_row) & (col < T))
        o_lat = _norm_rows(acc_scr[...], l_scr[...])
        _finish_mla(o_lat, T8, wuv_ref, g_ref, o_ref.at[0])


def _mla_sample(pt_flat, q_s, cache_mla, new_rows, wuv_p, gain, layer, DB, n_pages, T):
    ppc = 64 if n_pages % 64 == 0 else n_pages
    nch = n_pages // ppc
    page_rows = cache_mla.shape[2]
    rows = ppc * page_rows
    sub = 2048 if rows % 2048 == 0 else rows
    R = H_A * T8
    grid_spec = pltpu.PrefetchScalarGridSpec(
        num_scalar_prefetch=1,
        grid=(DB, nch),
        in_specs=[pl.BlockSpec((1, R, MLA_W), lambda b, c, pt: (b, 0, 0)),
                  pl.BlockSpec(memory_space=pl.ANY),
                  pl.BlockSpec((1, T8, MLA_W), lambda b, c, pt: (b, 0, 0)),
                  pl.BlockSpec(wuv_p.shape, lambda b, c, pt: (0, 0, 0)),
                  pl.BlockSpec(gain.shape, lambda b, c, pt: (0, 0))],
        out_specs=pl.BlockSpec((1, T8, H_A * V_DIM), lambda b, c, pt: (b, 0, 0)),
        scratch_shapes=[pltpu.VMEM((2, rows, MLA_W), _F32),
                        pltpu.SemaphoreType.DMA((2,)),
                        pltpu.VMEM((R, 1), _F32), pltpu.VMEM((R, 1), _F32), pltpu.VMEM((R, KV_LORA), _F32)])
    return pl.pallas_call(
        functools.partial(_mla_s_kernel, layer=layer, ppc=ppc, sub=sub, n_past=n_pages * page_rows, T=T),
        grid_spec=grid_spec,
        out_shape=jax.ShapeDtypeStruct((DB, T8, H_A * V_DIM), _BF),
        compiler_params=_cparams("arbitrary", "arbitrary"),
    )(pt_flat, q_s, cache_mla, new_rows, wuv_p, gain)


def _compress_s_kernel(pt_ref, cache_ref, pe_ref, w_ref, o_ref, buf, sem, *, layer, ppc, nsb):
    slot = _paged_prologue(pt_ref, cache_ref, layer, buf, sem, ppc)
    acc = _compress_rows(lambda half, start, stride: buf[slot, half, pl.ds(start, nsb, stride=stride), :],
                         pe_ref, w_ref, nsb)
    o_ref[0, 0] = acc[0:nsb]
    o_ref[0, 1] = acc[nsb:2 * nsb]


def _compress_sample(pt_flat, cache_cmp, pe_rows, w_bd, layer, DB, n_pages):
    ppc = 64 if n_pages % 64 == 0 else n_pages
    nch = n_pages // ppc
    page_rows = cache_cmp.shape[2]
    rows = ppc * page_rows
    nsb = rows // SEL_BLOCK
    nsb_all = nch * nsb
    grid_spec = pltpu.PrefetchScalarGridSpec(
        num_scalar_prefetch=1,
        grid=(DB, nch),
        in_specs=[pl.BlockSpec(memory_space=pl.ANY),
                  pl.BlockSpec(pe_rows.shape, lambda b, c, pt: (0, 0)),
                  pl.BlockSpec(w_bd.shape, lambda b, c, pt: (0, 0, 0))],
        out_specs=pl.BlockSpec((1, 2, nsb, KV_W), lambda b, c, pt: (b, 0, c, 0)),
        scratch_shapes=[pltpu.VMEM((2, KV_W // LANES, rows, LANES), _F32), pltpu.SemaphoreType.DMA((2,))])
    return pl.pallas_call(
        functools.partial(_compress_s_kernel, layer=layer, ppc=ppc, nsb=nsb),
        grid_spec=grid_spec,
        out_shape=jax.ShapeDtypeStruct((DB, 2, nsb_all, KV_W), _F32),
        compiler_params=_cparams("arbitrary", "arbitrary"),
    )(pt_flat, cache_cmp, pe_rows, w_bd)


def _cmp_topk_s_kernel(qb_ref, kc_ref, oc_ref, idx_ref, *, nsb, n_past, n_pick):
    qb = qb_ref[0]
    kcat = jnp.concatenate([kc_ref[0, 0], kc_ref[0, 1]], axis=0)
    lane = lax.broadcasted_iota(jnp.int32, (T8, LANES), 1)
    j = lax.broadcasted_iota(jnp.int32, (T8, nsb), 1)
    t_tok = n_past + lax.broadcasted_iota(jnp.int32, (T8, 1), 0)
    cur = t_tok // SEL_BLOCK
    for g in range(G_B):
        qp = _pad_group_queries(qb, g, T8)
        slope, r_tok = _row_consts(g, T8)
        o_c, p_c = _cmp_branch(qp, kcat, slope, n_past + r_tok, nsb)
        oc_ref[0, g] = o_c
        ph = p_c[0:T8]
        for m in range(1, HG):
            ph = ph + p_c[m * T8:(m + 1) * T8]
        imp = ph[:, 0:nsb] + ph[:, nsb:2 * nsb]
        forced = (j == 0) | (j == cur) | (j == cur - 1)
        v = jnp.where(forced, FORCE, imp)
        v = jnp.where(j * SEL_BLOCK <= t_tok, v, NEG_INF)
        _, picks = _topk_select(v, n_pick)
        out = jnp.zeros((T8, LANES), _F32)
        for k, idx in enumerate(picks):
            out = jnp.where(lane == k, idx, out)
        idx_ref[0, g] = out.astype(jnp.int32)


def _cmp_topk_sample(qb_s, kc_s, DB, n_past, n_pick):
    nsb = kc_s.shape[2]
    return pl.pallas_call(
        functools.partial(_cmp_topk_s_kernel, nsb=nsb, n_past=n_past, n_pick=n_pick),
        grid=(DB,),
        in_specs=[pl.BlockSpec((1, T8, H_B * HD_B), lambda b: (b, 0, 0)),
                  pl.BlockSpec((1, 2, nsb, KV_W), lambda b: (b, 0, 0, 0))],
        out_specs=[pl.BlockSpec((1, G_B, HG * T8, LANES), lambda b: (b, 0, 0, 0)),
                   pl.BlockSpec((1, G_B, T8, LANES), lambda b: (b, 0, 0, 0))],
        out_shape=[jax.ShapeDtypeStruct((DB, G_B, HG * T8, LANES), _F32),
                   jax.ShapeDtypeStruct((DB, G_B, T8, LANES), jnp.int32)],
        compiler_params=_cparams("arbitrary"),
    )(qb_s, kc_s)


def _sel_copy(cache_ref, layer, page, row0, buf, slot, inst, k, sem):
    return pltpu.make_async_copy(cache_ref.at[layer, page, pl.ds(row0, SEL_BLOCK)],
                                 buf.at[slot, inst, pl.ds(k * SEL_BLOCK, SEL_BLOCK)],
                                 sem.at[slot])


def _sel_win_s_kernel(idx_ref, pt_ref, qb_ref, gate_ref, oc_ref, cache_ref, new_ref, win_ref, gain_ref, o_ref,
                      buf, sem, *, layer, n_pages, n_past, T, n_pick):
    b = pl.program_id(0)
    nb = pl.num_programs(0)
    slot = b % 2
    n_inst = G_B * T
    n_copies = n_inst * n_pick
    page_rows = cache_ref.shape[2]
    blocks_per_page = page_rows // SEL_BLOCK

    def start_all(bb, sl):
        def issue(i, carry):
            inst = i // n_pick
            k = i % n_pick
            blk = idx_ref[bb * n_copies + i]
            page = pt_ref[bb * n_pages + blk // blocks_per_page]
            row0 = pl.multiple_of((blk % blocks_per_page) * SEL_BLOCK, SEL_BLOCK)
            _sel_copy(cache_ref, layer, page, row0, buf, sl, inst, k, sem).start()
            return carry
        lax.fori_loop(0, n_copies, issue, 0)

    @pl.when(b == 0)
    def _():
        start_all(b, slot)

    @pl.when(b + 1 < nb)
    def _():
        start_all(b + 1, 1 - slot)

    def wait_one(i, carry):
        _sel_copy(cache_ref, layer, 0, 0, buf, slot, 0, 0, sem).wait()
        return carry
    lax.fori_loop(0, n_copies, wait_one, 0)

    R = HG * T8
    scale = HD_B ** -0.5
    qb = qb_ref[0]
    gate = gate_ref[0]
    knew = new_ref[0]
    kwin = win_ref[0]
    wlen = kwin.shape[0]
    nk = n_pick * SEL_BLOCK
    o_heads = []
    for g in range(G_B):
        qp = _pad_group_queries(qb, g, T8)
        slope, r_tok = _row_consts(g, T8)
        t_row = n_past + r_tok
        s_n = _dot_nt(qp, knew[:, 0:LANES].astype(_BF)) * scale
        coln = lax.broadcasted_iota(jnp.int32, (R, T8), 1)
        s_n = s_n - slope * (r_tok - coln).astype(_F32)
        mask_n = (coln <= r_tok) & (coln < T)
        s_n = jnp.where(mask_n, s_n, NEG_INF)
        m_n = jnp.max(s_n, axis=-1, keepdims=True)
        v_new = knew[:, LANES:KV_W].astype(_BF)
        o_s = jnp.zeros((R, LANES), _F32)
        lane_blk = lax.broadcasted_iota(jnp.int32, (1, nk), 1) // SEL_BLOCK
        lane_off = lax.broadcasted_iota(jnp.int32, (1, nk), 1) % SEL_BLOCK
        for t in range(T):
            inst = g * T + t
            kv = buf[slot, inst]
            pos = lane_off
            for k in range(n_pick):
                blk = idx_ref[b * n_copies + inst * n_pick + k]
                pos = jnp.where(lane_blk == k, blk * SEL_BLOCK + lane_off, pos)
            s = _dot_nt(qp, kv[:, 0:LANES].astype(_BF)) * scale
            s = s - slope * (t_row - pos).astype(_F32)
            m = jnp.maximum(jnp.max(s, axis=-1, keepdims=True), m_n)
            e = jnp.exp(s - m)
            e_n = jnp.where(mask_n, jnp.exp(s_n - m), 0.0)
            l = jnp.sum(e, axis=-1, keepdims=True) + jnp.sum(e_n, axis=-1, keepdims=True)
            o = (_dot(e.astype(_BF), kv[:, LANES:KV_W].astype(_BF)) + _dot(e_n.astype(_BF), v_new)) / l
            o_s = jnp.where(r_tok == t, o, o_s)
        s = _dot_nt(qp, kwin[:, 0:LANES].astype(_BF)) * scale
        colw = lax.broadcasted_iota(jnp.int32, (R, wlen), 1)
        dist = r_tok + (wlen - T8) - colw
        s = s - slope * dist.astype(_F32)
        e, l = _masked_softmax(s, (dist >= 0) & (dist < WINDOW) & (colw < wlen - T8 + T))
        o_w = _norm_rows(_dot(e.astype(_BF), kwin[:, LANES:KV_W].astype(_BF)), l)
        o_c = oc_ref[0, g]
        heads = []
        for m in range(HG):
            hh = g * HG + m
            rs = slice(m * T8, (m + 1) * T8)
            heads.append(gate[:, hh:hh + 1] * o_c[rs]
                         + gate[:, H_B + hh:H_B + hh + 1] * o_s[rs]
                         + gate[:, 2 * H_B + hh:2 * H_B + hh + 1] * o_w[rs])
        o_heads.append(heads)
    _finish_nsa(o_heads, gain_ref, o_ref.at[0], T8)


def _sel_win_sample(idx_flat, pt_flat, qb_s, gate_s, oc_s, cache_sel, sel_new, win_all, gain,
                    layer, DB, n_pages, n_past, T, n_pick):
    wlen = win_all.shape[1]
    grid_spec = pltpu.PrefetchScalarGridSpec(
        num_scalar_prefetch=2,
        grid=(DB,),
        in_specs=[pl.BlockSpec((1, T8, H_B * HD_B), lambda b, *_: (b, 0, 0)),
                  pl.BlockSpec((1, T8, LANES), lambda b, *_: (b, 0, 0)),
                  pl.BlockSpec((1, G_B, HG * T8, LANES), lambda b, *_: (b, 0, 0, 0)),
                  pl.BlockSpec(memory_space=pl.ANY),
                  pl.BlockSpec((1, T8, KV_W), lambda b, *_: (b, 0, 0)),
                  pl.BlockSpec((1, wlen, KV_W), lambda b, *_: (b, 0, 0)),
                  pl.BlockSpec(gain.shape, lambda b, *_: (0, 0))],
        out_specs=pl.BlockSpec((1, T8, H_B * HD_B), lambda b, *_: (b, 0, 0)),
        scratch_shapes=[pltpu.VMEM((2, G_B * T, n_pick * SEL_BLOCK, KV_W), _F32),
                        pltpu.SemaphoreType.DMA((2,))])
    return pl.pallas_call(
        functools.partial(_sel_win_s_kernel, layer=layer, n_pages=n_pages, n_past=n_past, T=T, n_pick=n_pick),
        grid_spec=grid_spec,
        out_shape=jax.ShapeDtypeStruct((DB, T8, H_B * HD_B), _BF),
        compiler_params=_cparams("arbitrary"),
    )(idx_flat, pt_flat, qb_s, gate_s, oc_s, cache_sel, sel_new, win_all, gain)


def _post_kernel(x_ref, oa_ref, ob_ref, woa_ref, wob_ref, g1_ref, sc_ref, sh_ref, g2_ref,
                 wg_ref, wu_ref, wd_ref, fg_ref, o_ref, x1_scr, h_scr, acc_scr, *, final):
    j = pl.program_id(1)

    @pl.when(j == 0)
    def _():
        o = _dot(oa_ref[...], woa_ref[...]) + _dot(ob_ref[...], wob_ref[...])
        x1 = x_ref[...] + g1_ref[0] * o
        x1_scr[...] = x1
        h = x1 * lax.rsqrt(jnp.mean(x1 * x1, axis=-1, keepdims=True) + EPS) * (1.0 + sc_ref[0]) + sh_ref[0]
        h_scr[...] = h.astype(_BF)
        acc_scr[...] = jnp.zeros(acc_scr.shape, _F32)

    h = h_scr[...]
    gt = _dot(h, wg_ref[...])
    up = _dot(h, wu_ref[...])
    a = gt * jax.nn.sigmoid(gt) * up
    acc_scr[...] += _dot(a.astype(_BF), wd_ref[...])

    @pl.when(j == pl.num_programs(1) - 1)
    def _():
        y = x1_scr[...] + g2_ref[0] * acc_scr[...]
        if final:
            y = y * lax.rsqrt(jnp.mean(y * y, axis=-1, keepdims=True) + EPS) * fg_ref[...]
        o_ref[...] = y


def _post(x, oa, ob, woa, wob, g1, sc2, sh2, g2, wg, wu, wd, fg, tm, mod_map, final):
    N, D = x.shape
    F = wg.shape[1]
    nf = 2
    tf = F // nf
    mod_blk = (1, 1 if g1.shape[1] == 1 else tm, D)
    row = lambda i, j: (i, 0)
    mm = lambda i, j: mod_map(i)
    return pl.pallas_call(
        functools.partial(_post_kernel, final=final),
        grid=(N // tm, nf),
        in_specs=[pl.BlockSpec((tm, D), row),
                  pl.BlockSpec((tm, oa.shape[1]), row),
                  pl.BlockSpec((tm, ob.shape[1]), row),
                  pl.BlockSpec(woa.shape, lambda i, j: (0, 0)),
                  pl.BlockSpec(wob.shape, lambda i, j: (0, 0)),
                  pl.BlockSpec(mod_blk, mm), pl.BlockSpec(mod_blk, mm),
                  pl.BlockSpec(mod_blk, mm), pl.BlockSpec(mod_blk, mm),
                  pl.BlockSpec((D, tf), lambda i, j: (0, j)),
                  pl.BlockSpec((D, tf), lambda i, j: (0, j)),
                  pl.BlockSpec((tf, D), lambda i, j: (j, 0)),
                  pl.BlockSpec(fg.shape, lambda i, j: (0, 0))],
        out_specs=pl.BlockSpec((tm, D), row),
        out_shape=jax.ShapeDtypeStruct((N, D), _F32),
        scratch_shapes=[pltpu.VMEM((tm, D), _F32), pltpu.VMEM((tm, D), _BF), pltpu.VMEM((tm, D), _F32)],
        compiler_params=_cparams("arbitrary", "arbitrary"),
    )(x, oa, ob, woa, wob, g1, sc2, sh2, g2, wg, wu, wd, fg)


def _pair_heads(a, axis):
    shp = a.shape
    a = a.reshape(shp[:axis] + (G_B, HG, HD_B) + shp[axis + 1:])
    a = jnp.swapaxes(a, axis, axis + 1)
    return a.reshape(shp)


def _pack_w_in(w_in):
    D = w_in.shape[0]
    i1 = Q_LORA
    i2 = i1 + KV_LORA
    i3 = i2 + ROPE_DIM
    i4 = i3 + H_B * HD_B
    i5 = i4 + N_BRANCH * KV_W
    half = ROPE_DIM // 2
    kr = w_in[:, i2:i3]
    krr = jnp.concatenate([-kr[:, half:], kr[:, :half]], axis=1)
    gate = w_in[:, i5:].reshape(D, H_B, N_BRANCH).transpose(0, 2, 1).reshape(D, N_BRANCH * H_B)
    z = lambda n: jnp.zeros((D, n), w_in.dtype)
    cols = [w_in[:, :i2], kr, z(C_KRR - C_KR - ROPE_DIM), krr, z(C_QB - C_KRR - ROPE_DIM),
            _pair_heads(w_in[:, i3:i4], 1), w_in[:, i4:i5], gate, z(C_END - C_GATE - N_BRANCH * H_B)]
    return jnp.concatenate(cols, axis=1).astype(_BF)


def _pack_w_q(w_uq, w_uk):
    half = ROPE_DIM // 2
    w = w_uq.reshape(Q_LORA, H_A, NOPE_DIM + ROPE_DIM)
    nope_t = jnp.transpose(w[:, :, :NOPE_DIM], (1, 0, 2))
    w_uk_t = jnp.transpose(w_uk, (1, 0, 2))
    w_abs = jnp.transpose(_fold_q_absorb(nope_t, w_uk_t), (1, 0, 2)).reshape(Q_LORA, H_A * KV_LORA)
    rope = w[:, :, NOPE_DIM:]
    rot = jnp.concatenate([-rope[:, :, half:], rope[:, :, :half]], axis=2)
    return jnp.concatenate([w_abs, rope.reshape(Q_LORA, -1), rot.reshape(Q_LORA, -1)], axis=1).astype(_BF)


def _pack_w_uv(w_uv):
    w = jnp.transpose(w_uv, (1, 0, 2))
    zero = jnp.zeros_like(w)
    even = jnp.concatenate([w, zero], axis=2)
    odd = jnp.concatenate([zero, w], axis=2)
    is_even = (jnp.arange(H_A) % 2 == 0)[:, None, None]
    return jnp.where(is_even, even, odd).astype(_BF)


def _pack_w_cmp(w_cmp, cmp_pe):
    blocks = []
    for kv in range(2):
        for g in range(G_B):
            idx = kv * G_B + g
            row = [w_cmp[kv] if c == idx else jnp.zeros_like(w_cmp[kv]) for c in range(2 * G_B)]
            blocks.append(jnp.concatenate(row, axis=2))
    w_bd = jnp.concatenate(blocks, axis=1).astype(_BF)
    pe_rows = jnp.concatenate([cmp_pe[0]] * G_B + [cmp_pe[1]] * G_B, axis=1)
    return w_bd, pe_rows


def _rope_tables(pos):
    inv = ROPE_THETA ** (-jnp.arange(0, ROPE_DIM, 2, dtype=_F32) / ROPE_DIM)
    ang = pos.astype(_F32)[:, None] * inv[None, :]
    reps = LANES // (ROPE_DIM // 2)
    return jnp.tile(jnp.cos(ang), (1, reps)), jnp.tile(jnp.sin(ang), (1, reps))


def kernel(x_prompt, x_sample, c_prompt, c_sample, cache_mla, cache_nsa_cmp, cache_nsa_sel, state_nsa_win,
           page_table, w_mod, b_mod, w_in, q_a_g, kv_a_g, w_uq, w_uk, w_uv, cmp_pe, w_cmp, out_a_g, out_b_g,
           w_o, w_ff_gate, w_ff_up, w_ff_down, final_g):
    B, S, D = x_prompt.shape
    DB, T, _ = x_sample.shape
    depth = w_in.shape[0]
    n_pages = page_table.shape[1]
    page_rows = cache_mla.shape[2]
    n_past = n_pages * page_rows
    wb = state_nsa_win.shape[2]
    assert S % QBLK == 0 and S >= WINDOW + QBLK and T <= T8 and n_past % SEL_BLOCK == 0
    assert wb == WINDOW and n_past // SEL_BLOCK + 1 >= SEL_TOPK
    n_pick = SEL_TOPK - 1
    Np, Ns = B * S, DB * T
    tm_p = 256
    tm_s = min(256, Ns)
    tmf_p = 512 if S % 512 == 0 else 256
    tmf_s = min(512, Ns)

    pt_flat = page_table.reshape(-1).astype(jnp.int32)
    cache_cmp = cache_nsa_cmp.reshape(cache_nsa_cmp.shape[:3] + (KV_W,))
    cache_sel = cache_nsa_sel.reshape(cache_nsa_sel.shape[:3] + (KV_W,))
    cs_p, sn_p = _rope_tables(jnp.arange(S))
    cs_s, sn_s = _rope_tables(jnp.tile(n_past + jnp.arange(T), DB))
    c_all = jnp.concatenate([c_prompt, c_sample], axis=0)
    xp = x_prompt.reshape(Np, D)
    xs = x_sample.reshape(Ns, D)
    fg = final_g.reshape(1, D)

    def pad_t(a):
        return jnp.pad(a, ((0, 0), (0, T8 - T), (0, 0)))

    st_p, st_s = [], []
    for l in range(depth):
        w_in_p = _pack_w_in(w_in[l])
        wq_all = _pack_w_q(w_uq[l], w_uk[l])
        wuv_p = _pack_w_uv(w_uv[l])
        w_bd, pe_rows = _pack_w_cmp(w_cmp[l], cmp_pe[l])
        qg = q_a_g[l].reshape(1, Q_LORA)
        kvg = kv_a_g[l].reshape(1, KV_LORA)
        ga = out_a_g[l].reshape(1, -1)
        gb = _pair_heads(out_b_g[l], 0).reshape(1, -1)
        woa = w_o[l, :H_A * V_DIM].astype(_BF)
        wob = _pair_heads(w_o[l, H_A * V_DIM:], 0).astype(_BF)
        wg = w_ff_gate[l].astype(_BF)
        wu = w_ff_up[l].astype(_BF)
        wd = w_ff_down[l].astype(_BF)
        final = l == depth - 1

        mod = _mod(c_all, w_mod, b_mod, l)
        mods_p = [m.reshape(B, 1, D) for m in jnp.split(mod[:B], 6, axis=1)]
        mods_s = [jnp.repeat(m, T, axis=0).reshape(1, Ns, D) for m in jnp.split(mod[B:], 6, axis=1)]

        tpb = S // tm_p
        qmla, mla_row, qb, cmp_r, sel_r, win_r, gate = _proj(
            xp, mods_p[1], mods_p[0], cs_p, sn_p, w_in_p, qg, kvg, wq_all, tm_p,
            lambda i: (i // tpb, 0, 0), lambda i: (i % tpb, 0))
        oa = _mla_prompt(qmla, mla_row, wuv_p, ga, B, S)
        kc_p = _compress_prompt(cmp_r, pe_rows, w_bd, B, S)
        ob = _nsa_prompt(qb, gate, kc_p, sel_r, win_r, gb, B, S)
        tpbf = S // tmf_p
        xp = _post(xp, oa, ob, woa, wob, mods_p[2], mods_p[4], mods_p[3], mods_p[5], wg, wu, wd, fg, tmf_p,
                   lambda i: (i // tpbf, 0, 0), final)
        wbp = min(WINDOW, S)
        st_p.append((mla_row.reshape(B, S, MLA_W),
                     cmp_r.reshape(B, S, 2, G_B, HD_B), sel_r.reshape(B, S, 2, G_B, HD_B),
                     win_r.reshape(B, S, 2, G_B, HD_B)[:, S - wbp:]))

        qmla_s, mla_s, qb_s, cmp_s, sel_s, win_s, gate_s = _proj(
            xs, mods_s[1], mods_s[0], cs_s, sn_s, w_in_p, qg, kvg, wq_all, tm_s,
            lambda i: (0, i, 0), lambda i: (i, 0))
        q_s = pad_t(jnp.transpose(qmla_s.reshape(H_A, DB, T, MLA_W), (1, 0, 2, 3)).reshape(DB * H_A, T, MLA_W))
        q_s = q_s.reshape(DB, H_A * T8, MLA_W)
        oa_s = _mla_sample(pt_flat, q_s, cache_mla, pad_t(mla_s.reshape(DB, T, MLA_W)), wuv_p, ga,
                           l, DB, n_pages, T)
        kc_s = _compress_sample(pt_flat, cache_cmp, pe_rows, w_bd, l, DB, n_pages)
        qb_s8 = pad_t(qb_s.reshape(DB, T, -1))
        gate_s8 = pad_t(gate_s.reshape(DB, T, -1))
        oc_s, idx_s = _cmp_topk_sample(qb_s8, kc_s, DB, n_past, n_pick)
        idx_flat = idx_s[:, :, :T, :n_pick].reshape(-1)
        win_all = jnp.concatenate([state_nsa_win[l].reshape(DB, wb, KV_W), pad_t(win_s.reshape(DB, T, KV_W))], axis=1)
        ob_s = _sel_win_sample(idx_flat, pt_flat, qb_s8, gate_s8, oc_s, cache_sel, pad_t(sel_s.reshape(DB, T, KV_W)),
                               win_all, gb, l, DB, n_pages, n_past, T, n_pick)
        oa_s = oa_s[:, :T].reshape(Ns, -1)
        ob_s = ob_s[:, :T].reshape(Ns, -1)
        xs = _post(xs, oa_s, ob_s, woa, wob, mods_s[2], mods_s[4], mods_s[3], mods_s[5], wg, wu, wd, fg, tmf_s,
                   lambda i: (0, i, 0), final)
        st_s.append((mla_s.reshape(DB, T, MLA_W),
                     cmp_s.reshape(DB, T, 2, G_B, HD_B), sel_s.reshape(DB, T, 2, G_B, HD_B),
                     win_all[:, T:T + wb].reshape(DB, wb, 2, G_B, HD_B)))

    stack = lambda sts, i: jnp.stack([s[i] for s in sts], axis=0)
    return (xp.reshape(B, S, D), xs.reshape(DB, T, D), stack(st_p, 0), stack(st_s, 0), stack(st_p, 1), stack(st_s, 1),
            stack(st_p, 2), stack(st_s, 2), stack(st_p, 3), stack(st_s, 3))
```
